```python
import jax
import jax.numpy as jnp
from jax import lax
import numpy as np

D_MODEL = 1024
BATCH = 8
SEQ = 2048
DEPTH = 2
DEC_BATCH = 32
DEC_SEQ = 16
PAST_LEN = 2048

CHUNK = 64
Q_BLOCK = 128
N_EVEN = (DEPTH + 1) // 2
N_ODD = DEPTH // 2
DN_ALPHA = (2.0 * DEPTH) ** 0.25
DN_BETA = (8.0 * DEPTH) ** -0.25
LN_EPS = 1e-5
NORM_EPS = 1e-6
ROPE_THETA = 500000.0

MLA_HEADS = 8
MLA_Q_LORA = 256
MLA_KV_LORA = 256
MLA_NOPE = 64
MLA_ROPE = 32
MLA_V = 64
MLA_SCALE = (MLA_NOPE + MLA_ROPE) ** -0.5

RET_HEADS = 8
RET_DK = 64
RET_DV = 64
RET_THETA = 10000.0

GLA_HEADS = 4
GLA_DK = 64
GLA_DV = 128
GLA_LOW_RANK = 16
GLA_GATE_NORM = 16.0

SWA_HEADS = 8
SWA_KV_HEADS = 2
SWA_GROUP = SWA_HEADS // SWA_KV_HEADS
SWA_HD = 64
WINDOW = 128
WINDOW_CHUNKS = WINDOW // CHUNK
SWA_ROT = SWA_HD // 4
SWA_SCALE = SWA_HD ** -0.5

D_FF = 2816
CONV_W = 3

EVEN_SIZES = (MLA_Q_LORA, MLA_KV_LORA, MLA_ROPE, RET_HEADS * RET_DK, RET_HEADS * RET_DK, RET_HEADS * RET_DV, RET_HEADS * RET_DV)
ODD_SIZES = (GLA_HEADS * GLA_DK, GLA_HEADS * GLA_DK, GLA_HEADS * GLA_DV, GLA_HEADS * GLA_DV, GLA_LOW_RANK, SWA_HEADS * SWA_HD, SWA_KV_HEADS * SWA_HD, SWA_KV_HEADS * SWA_HD)
IN_EVEN = sum(EVEN_SIZES)
IN_ODD = sum(ODD_SIZES)
MIX_EVEN = MLA_HEADS * MLA_V + RET_HEADS * RET_DV
MIX_ODD = GLA_HEADS * GLA_DV + SWA_HEADS * SWA_HD

kernel_name = 'hybrid_chunk_streaming_encoder_step'


def _split(h, sizes):
    return jnp.split(h, np.cumsum(sizes)[:-1].tolist(), axis=-1)


def layer_norm(x, g, b):
    xf = x.astype(jnp.float32)
    xc = xf - jnp.mean(xf, -1, keepdims=True)
    var = jnp.mean(xc * xc, -1, keepdims=True)
    return (xc * lax.rsqrt(var + LN_EPS) * g.astype(jnp.float32) + b.astype(jnp.float32)).astype(x.dtype)


def rms_norm(x, g):
    xf = x.astype(jnp.float32)
    y = xf * lax.rsqrt(jnp.mean(xf * xf, -1, keepdims=True) + NORM_EPS)
    return (y * g.astype(jnp.float32)).astype(x.dtype)


def head_norm(o, center):
    if center:
        o = o - jnp.mean(o, -1, keepdims=True)
    return o * lax.rsqrt(jnp.mean(o * o, -1, keepdims=True) + NORM_EPS)


def rope_tables(pos, dim, theta):
    inv = 1.0 / (theta ** (jnp.arange(0, dim, 2, dtype=jnp.float32) / dim))
    ang = pos.astype(jnp.float32)[:, None] * inv[None, :]
    return jnp.cos(ang), jnp.sin(ang)


def apply_rope(x, cos, sin):
    x1, x2 = jnp.split(x.astype(jnp.float32), 2, axis=-1)
    c, s = cos[:, None, :], sin[:, None, :]
    return jnp.concatenate([x1 * c - x2 * s, x1 * s + x2 * c], -1).astype(x.dtype)


def chunk_causal_attention(q, k, v, scale):
    B, S, H, Dk = q.shape
    nb = S // Q_BLOCK
    qb = q.reshape(B, nb, Q_BLOCK, H, Dk).swapaxes(0, 1)
    key_chunk = jnp.arange(S) // CHUNK

    def one_block(args):
        i, qi = args
        s = jnp.einsum('bqhd,bkhd->bhqk', qi, k, preferred_element_type=jnp.float32) * scale
        q_chunk = (i * Q_BLOCK + jnp.arange(Q_BLOCK)) // CHUNK
        s = jnp.where(key_chunk[None, :] <= q_chunk[:, None], s, -jnp.inf)
        p = jax.nn.softmax(s, axis=-1).astype(v.dtype)
        return jnp.einsum('bhqk,bkhd->bqhd', p, v)

    out = lax.map(one_block, (jnp.arange(nb), qb))
    return out.swapaxes(0, 1).reshape(B, S, H, v.shape[-1])


def mla_queries(q_lat, pos, q_norm, w_uq):
    B, T, _ = q_lat.shape
    q = (rms_norm(q_lat, q_norm) @ w_uq).reshape(B, T, MLA_HEADS, MLA_NOPE + MLA_ROPE)
    cos, sin = rope_tables(pos, MLA_ROPE, ROPE_THETA)
    return q[..., :MLA_NOPE], apply_rope(q[..., MLA_NOPE:], cos, sin)


def mla_latent_rows(kv_lat, kpe_raw, pos, kv_norm):
    cos, sin = rope_tables(pos, MLA_ROPE, ROPE_THETA)
    return rms_norm(kv_lat, kv_norm), apply_rope(kpe_raw[:, :, None, :], cos, sin)[:, :, 0]


def mla_prompt(q_nope, q_pe, ckv, kpe, w_ukv):
    B, S, _ = ckv.shape
    kv = (ckv @ w_ukv).reshape(B, S, MLA_HEADS, MLA_NOPE + MLA_V)
    k = jnp.concatenate([kv[..., :MLA_NOPE], jnp.broadcast_to(kpe[:, :, None, :], (B, S, MLA_HEADS, MLA_ROPE))], -1)
    q = jnp.concatenate([q_nope, q_pe], -1)
    return chunk_causal_attention(q, k, kv[..., MLA_NOPE:], MLA_SCALE)


def mla_sample(q_nope, q_pe, ckv_all, kpe_all, w_ukv):
    w = w_ukv.reshape(MLA_KV_LORA, MLA_HEADS, MLA_NOPE + MLA_V)
    q_abs = jnp.einsum('bthn,chn->bthc', q_nope, w[..., :MLA_NOPE])
    s = (jnp.einsum('bthc,bkc->bhtk', q_abs, ckv_all, preferred_element_type=jnp.float32)
         + jnp.einsum('bthr,bkr->bhtk', q_pe, kpe_all, preferred_element_type=jnp.float32)) * MLA_SCALE
    p = jax.nn.softmax(s, axis=-1).astype(ckv_all.dtype)
    o_lat = jnp.einsum('bhtk,bkc->bthc', p, ckv_all)
    return jnp.einsum('bthc,chv->bthv', o_lat, w[..., MLA_NOPE:])


def linear_chunk_step(S, q, k, v, g):
    L = q.shape[1]
    qf, kf, vf = q.astype(jnp.float32), k.astype(jnp.float32), v.astype(jnp.float32)
    b = jnp.cumsum(g, axis=1)
    b_last = b[:, -1]
    o_inter = jnp.einsum('blhk,bhkv->blhv', qf * jnp.exp(b), S)
    causal = jnp.tril(jnp.ones((L, L), dtype=bool))
    diff = b[:, :, None] - b[:, None, :]
    if g.shape[-1] == 1:
        decay = jnp.exp(jnp.where(causal[None, :, :, None], diff[..., 0], -jnp.inf))
        A = jnp.einsum('bnhk,bmhk,bnmh->bhnm', qf, kf, decay)
    else:
        decay = jnp.exp(jnp.where(causal[None, :, :, None, None], diff, -jnp.inf))
        A = jnp.einsum('bnhk,bmhk,bnmhk->bhnm', qf, kf, decay)
    o_intra = jnp.einsum('bhnm,bmhv->bnhv', A, vf)
    S_new = jnp.exp(b_last)[..., None] * S + jnp.einsum('bmhk,bmhv->bhkv', kf * jnp.exp(b_last[:, None] - b), vf)
    return S_new, o_inter + o_intra


def linear_attention(q, k, v, g, S0):
    B, T = q.shape[:2]
    if T <= CHUNK:
        return linear_chunk_step(S0, q, k, v, g)
    nc = T // CHUNK
    xs = tuple(t.reshape(B, nc, CHUNK, *t.shape[2:]).swapaxes(0, 1) for t in (q, k, v, g))
    S, o = lax.scan(lambda s, xs_: linear_chunk_step(s, *xs_), S0, xs)
    return S, o.swapaxes(0, 1).reshape(B, T, *o.shape[3:])


def retention(rq, rk, rv, rg, pos, s0, norm_g):
    B, T, _ = rq.shape
    cos, sin = rope_tables(pos, RET_DK, RET_THETA)
    q = apply_rope(rq.reshape(B, T, RET_HEADS, RET_DK), cos, sin)
    k = apply_rope(rk.reshape(B, T, RET_HEADS, RET_DK), cos, sin) * (RET_DK ** -0.5)
    v = rv.reshape(B, T, RET_HEADS, RET_DV)
    log_gamma = jnp.log(1.0 - 2.0 ** (-5.0 - jnp.arange(RET_HEADS, dtype=jnp.float32)))
    g = jnp.broadcast_to(log_gamma[None, None, :, None], (B, T, RET_HEADS, 1))
    s, o = linear_attention(q, k, v, g, s0.astype(jnp.float32))
    y = head_norm(o, True).reshape(B, T, RET_HEADS * RET_DV) * norm_g.astype(jnp.float32)
    return (y * jax.nn.silu(rg.astype(jnp.float32))).astype(rq.dtype), s


def gla(gq, gk, gv, gg, ga, s0, w_a2, b_a, norm_g):
    B, T, _ = gq.shape
    q = gq.reshape(B, T, GLA_HEADS, GLA_DK) * (GLA_DK ** -0.5)
    k = gk.reshape(B, T, GLA_HEADS, GLA_DK)
    v = gv.reshape(B, T, GLA_HEADS, GLA_DV)
    logit = (ga @ w_a2 + b_a).astype(jnp.float32)
    g = (jax.nn.log_sigmoid(logit) / GLA_GATE_NORM).reshape(B, T, GLA_HEADS, GLA_DK)
    s, o = linear_attention(q, k, v, g, s0.astype(jnp.float32))
    y = head_norm(o, False).reshape(B, T, GLA_HEADS * GLA_DV) * norm_g.astype(jnp.float32)
    return (y * jax.nn.silu(gg.astype(jnp.float32))).astype(gq.dtype), s


def partial_rope(x, pos):
    cos, sin = rope_tables(pos, SWA_ROT, ROPE_THETA)
    return jnp.concatenate([apply_rope(x[..., :SWA_ROT], cos, sin), x[..., SWA_ROT:]], -1)


def sink_softmax(s, sinks_hg):
    sk = sinks_hg.astype(jnp.float32)[..., None, None]
    m = jnp.maximum(jnp.max(s, -1, keepdims=True), sk)
    p = jnp.exp(s - m)
    return p / (jnp.sum(p, -1, keepdims=True) + jnp.exp(sk - m))


def swa_prompt(q, k, v, sinks_hg):
    B, S, _, D = q.shape
    nc = S // CHUNK
    nk = (WINDOW_CHUNKS + 1) * CHUNK
    qc = q.reshape(B, nc, CHUNK, SWA_KV_HEADS, SWA_GROUP, D)
    pad = ((0, 0), (WINDOW, 0), (0, 0), (0, 0))
    kp = jnp.pad(k, pad).reshape(B, nc + WINDOW_CHUNKS, CHUNK, SWA_KV_HEADS, D)
    vp = jnp.pad(v, pad).reshape(B, nc + WINDOW_CHUNKS, CHUNK, SWA_KV_HEADS, D)
    kb = jnp.concatenate([kp[:, j:j + nc] for j in range(WINDOW_CHUNKS + 1)], axis=2)
    vb = jnp.concatenate([vp[:, j:j + nc] for j in range(WINDOW_CHUNKS + 1)], axis=2)
    s = jnp.einsum('bcqhgd,bckhd->bchgqk', qc, kb, preferred_element_type=jnp.float32) * SWA_SCALE
    key_chunk = jnp.arange(nc)[:, None] - WINDOW_CHUNKS + (jnp.arange(nk) // CHUNK)[None, :]
    s = jnp.where((key_chunk >= 0)[None, :, None, None, None, :], s, -jnp.inf)
    p = sink_softmax(s, sinks_hg).astype(v.dtype)
    o = jnp.einsum('bchgqk,bckhd->bcqhgd', p, vb)
    return o.reshape(B, S, SWA_HEADS * D)


def swa_sample(q, k_all, v_all, sinks_hg):
    B, T, _, D = q.shape
    qg = q.reshape(B, T, SWA_KV_HEADS, SWA_GROUP, D)
    s = jnp.einsum('bqhgd,bkhd->bhgqk', qg, k_all, preferred_element_type=jnp.float32) * SWA_SCALE
    p = sink_softmax(s, sinks_hg).astype(v_all.dtype)
    return jnp.einsum('bhgqk,bkhd->bqhgd', p, v_all).reshape(B, T, SWA_HEADS * D)


def even_mixer(x, pos, past_ckv, past_kpe, ret_s0, w_in, q_norm, w_uq, kv_norm, w_ukv, ret_norm, w_out):
    B, T, _ = x.shape
    q_lat, kv_lat, kpe_raw, rq, rk, rv, rg = _split(x @ w_in, EVEN_SIZES)
    q_nope, q_pe = mla_queries(q_lat, pos, q_norm, w_uq)
    ckv, kpe = mla_latent_rows(kv_lat, kpe_raw, pos, kv_norm)
    if past_ckv is None:
        a = mla_prompt(q_nope, q_pe, ckv, kpe, w_ukv)
    else:
        ckv_all = jnp.concatenate([past_ckv.astype(ckv.dtype), ckv], 1)
        kpe_all = jnp.concatenate([past_kpe.astype(kpe.dtype), kpe], 1)
        a = mla_sample(q_nope, q_pe, ckv_all, kpe_all, w_ukv)
    r, ret_s = retention(rq, rk, rv, rg, pos, ret_s0, ret_norm)
    out = jnp.concatenate([a.reshape(B, T, MLA_HEADS * MLA_V), r], -1) @ w_out
    return out, ckv, kpe, ret_s


def odd_mixer(x, pos, gla_s0, past_k, past_v, w_in, w_a2, b_a, gla_norm, sinks, w_out):
    B, T, _ = x.shape
    gq, gk, gv, gg, ga, sq, sk, sv = _split(x @ w_in, ODD_SIZES)
    y_gla, gla_s = gla(gq, gk, gv, gg, ga, gla_s0, w_a2, b_a, gla_norm)
    q = partial_rope(sq.reshape(B, T, SWA_HEADS, SWA_HD), pos)
    k = partial_rope(sk.reshape(B, T, SWA_KV_HEADS, SWA_HD), pos)
    v = sv.reshape(B, T, SWA_KV_HEADS, SWA_HD)
    sinks_hg = sinks.reshape(SWA_KV_HEADS, SWA_GROUP)
    if past_k is None:
        a = swa_prompt(q, k, v, sinks_hg)
        k_buf, v_buf = k[:, -WINDOW:], v[:, -WINDOW:]
    else:
        rows = past_k.shape[1]
        k_all = jnp.concatenate([past_k.astype(k.dtype), k], 1)
        v_all = jnp.concatenate([past_v.astype(v.dtype), v], 1)
        a = swa_sample(q, k_all, v_all, sinks_hg)
        k_buf, v_buf = k_all[:, -rows:], v_all[:, -rows:]
    out = jnp.concatenate([y_gla, a], -1) @ w_out
    return out, gla_s, k_buf, v_buf


def conv_ffn(x, conv_prev, w_up, conv_w, conv_b, w_down):
    B, T, _ = x.shape
    u = x @ w_up
    if conv_prev is None:
        conv_prev = jnp.zeros((B, CONV_W - 1, u.shape[-1]), u.dtype)
    ue = jnp.concatenate([conv_prev.astype(u.dtype), u], 1)
    c = conv_b
    for j in range(CONV_W):
        c = c + ue[:, j:j + T] * conv_w[j]
    a, b = jnp.split(c, 2, axis=-1)
    y = (jax.nn.silu(a) * b) @ w_down
    return y, ue[:, -(CONV_W - 1):]


def run_trunk(x, pos, st, w):
    B = x.shape[0]
    ckvs, kpes, rets, glas, kbufs, vbufs, convs = [], [], [], [], [], [], []
    for layer in range(DEPTH):
        t = layer // 2
        if layer % 2 == 0:
            if st is None:
                past_ckv, past_kpe = None, None
                s0 = jnp.zeros((B, RET_HEADS, RET_DK, RET_DV), jnp.float32)
            else:
                past_ckv, past_kpe, s0 = st['ckv'][t], st['kpe'][t], st['ret'][t]
            m, ckv, kpe, s = even_mixer(x, pos, past_ckv, past_kpe, s0, w['w_in_even'][t], w['mla_q_norm'][t],
                                        w['mla_w_uq'][t], w['mla_kv_norm'][t], w['mla_w_ukv'][t],
                                        w['ret_norm'][t], w['w_out_even'][t])
            ckvs.append(ckv)
            kpes.append(kpe)
            rets.append(s)
        else:
            if st is None:
                past_k, past_v = None, None
                s0 = jnp.zeros((B, GLA_HEADS, GLA_DK, GLA_DV), jnp.float32)
            else:
                past_k, past_v, s0 = st['swa_k'][t], st['swa_v'][t], st['gla'][t]
            m, s, kb, vb = odd_mixer(x, pos, s0, past_k, past_v, w['w_in_odd'][t], w['gla_w_a2'][t],
                                     w['gla_b_a'][t], w['gla_norm'][t], w['swa_sinks'][t], w['w_out_odd'][t])
            glas.append(s)
            kbufs.append(kb)
            vbufs.append(vb)
        x = layer_norm(DN_ALPHA * x + m, w['ln_mix_g'][layer], w['ln_mix_b'][layer])
        f, c = conv_ffn(x, None if st is None else st['conv'][layer], w['ffn_w_up'][layer],
                        w['ffn_conv_w'][layer], w['ffn_conv_b'][layer], w['ffn_w_down'][layer])
        convs.append(c)
        x = layer_norm(DN_ALPHA * x + f, w['ln_ffn_g'][layer], w['ln_ffn_b'][layer])
    return (x, jnp.stack(ckvs), jnp.stack(kpes), jnp.stack(rets), jnp.stack(glas),
            jnp.stack(kbufs), jnp.stack(vbufs), jnp.stack(convs))


def setup_inputs(seed: int = 0) -> dict:
    key = jax.random.key(seed)
    keys = list(jax.random.split(key, 40))

    def nrm(shape, scale):
        return jax.random.normal(keys.pop(), shape, jnp.float32) * scale

    def gain(shape):
        return 1.0 + nrm(shape, 0.02)

    swa_rows = min(WINDOW, PAST_LEN)
    return {
        'x_prompt': nrm((BATCH, SEQ, D_MODEL), 1.0),
        'x_sample': nrm((DEC_BATCH, DEC_SEQ, D_MODEL), 1.0),
        'cache_mla_ckv': nrm((N_EVEN, DEC_BATCH, PAST_LEN, MLA_KV_LORA), 1.0),
        'cache_mla_kpe': nrm((N_EVEN, DEC_BATCH, PAST_LEN, MLA_ROPE), 1.0),
        'state_ret': nrm((N_EVEN, DEC_BATCH, RET_HEADS, RET_DK, RET_DV), 0.3),
        'state_gla': nrm((N_ODD, DEC_BATCH, GLA_HEADS, GLA_DK, GLA_DV), 0.3),
        'cache_swa_k': nrm((N_ODD, DEC_BATCH, swa_rows, SWA_KV_HEADS, SWA_HD), 1.0),
        'cache_swa_v': nrm((N_ODD, DEC_BATCH, swa_rows, SWA_KV_HEADS, SWA_HD), 1.0),
        'state_ffn_conv': nrm((DEPTH, DEC_BATCH, CONV_W - 1, 2 * D_FF), 1.0),
        'w_in_even': nrm((N_EVEN, D_MODEL, IN_EVEN), D_MODEL ** -0.5),
        'mla_q_norm': gain((N_EVEN, MLA_Q_LORA)),
        'mla_w_uq': nrm((N_EVEN, MLA_Q_LORA, MLA_HEADS * (MLA_NOPE + MLA_ROPE)), MLA_Q_LORA ** -0.5),
        'mla_kv_norm': gain((N_EVEN, MLA_KV_LORA)),
        'mla_w_ukv': nrm((N_EVEN, MLA_KV_LORA, MLA_HEADS * (MLA_NOPE + MLA_V)), MLA_KV_LORA ** -0.5),
        'ret_norm': gain((N_EVEN, RET_HEADS * RET_DV)),
        'w_out_even': nrm((N_EVEN, MIX_EVEN, D_MODEL), DN_BETA * MIX_EVEN ** -0.5),
        'w_in_odd': nrm((N_ODD, D_MODEL, IN_ODD), D_MODEL ** -0.5),
        'gla_w_a2': nrm((N_ODD, GLA_LOW_RANK, GLA_HEADS * GLA_DK), GLA_LOW_RANK ** -0.5),
        'gla_b_a': nrm((N_ODD, GLA_HEADS * GLA_DK), 0.1),
        'gla_norm': gain((N_ODD, GLA_HEADS * GLA_DV)),
        'swa_sinks': nrm((N_ODD, SWA_HEADS), 0.5),
        'w_out_odd': nrm((N_ODD, MIX_ODD, D_MODEL), DN_BETA * MIX_ODD ** -0.5),
        'ffn_w_up': nrm((DEPTH, D_MODEL, 2 * D_FF), D_MODEL ** -0.5),
        'ffn_conv_w': nrm((DEPTH, CONV_W, 2 * D_FF), CONV_W ** -0.5),
        'ffn_conv_b': nrm((DEPTH, 2 * D_FF), 0.02),
        'ffn_w_down': nrm((DEPTH, D_FF, D_MODEL), DN_BETA * D_FF ** -0.5),
        'ln_mix_g': gain((DEPTH, D_MODEL)),
        'ln_mix_b': nrm((DEPTH, D_MODEL), 0.02),
        'ln_ffn_g': gain((DEPTH, D_MODEL)),
        'ln_ffn_b': nrm((DEPTH, D_MODEL), 0.02),
    }


def reference(x_prompt, x_sample, cache_mla_ckv, cache_mla_kpe, state_ret, state_gla, cache_swa_k, cache_swa_v,
              state_ffn_conv, w_in_even, mla_q_norm, mla_w_uq, mla_kv_norm, mla_w_ukv, ret_norm, w_out_even,
              w_in_odd, gla_w_a2, gla_b_a, gla_norm, swa_sinks, w_out_odd, ffn_w_up, ffn_conv_w, ffn_conv_b,
              ffn_w_down, ln_mix_g, ln_mix_b, ln_ffn_g, ln_ffn_b):
    w = {'w_in_even': w_in_even, 'mla_q_norm': mla_q_norm, 'mla_w_uq': mla_w_uq, 'mla_kv_norm': mla_kv_norm,
         'mla_w_ukv': mla_w_ukv, 'ret_norm': ret_norm, 'w_out_even': w_out_even, 'w_in_odd': w_in_odd,
         'gla_w_a2': gla_w_a2, 'gla_b_a': gla_b_a, 'gla_norm': gla_norm, 'swa_sinks': swa_sinks,
         'w_out_odd': w_out_odd, 'ffn_w_up': ffn_w_up, 'ffn_conv_w': ffn_conv_w, 'ffn_conv_b': ffn_conv_b,
         'ffn_w_down': ffn_w_down, 'ln_mix_g': ln_mix_g, 'ln_mix_b': ln_mix_b, 'ln_ffn_g': ln_ffn_g,
         'ln_ffn_b': ln_ffn_b}
    st = {'ckv': cache_mla_ckv, 'kpe': cache_mla_kpe, 'ret': state_ret, 'gla': state_gla,
          'swa_k': cache_swa_k, 'swa_v': cache_swa_v, 'conv': state_ffn_conv}
    pos_p = jnp.arange(x_prompt.shape[1])
    pos_s = cache_mla_ckv.shape[2] + jnp.arange(x_sample.shape[1])
    y_prompt, p_ckv, p_kpe, p_ret, p_gla, p_k, p_v, p_conv = run_trunk(x_prompt, pos_p, None, w)
    y_sample, s_ckv, s_kpe, s_ret, s_gla, s_k, s_v, s_conv = run_trunk(x_sample, pos_s, st, w)
    return (y_prompt, y_sample, p_ckv, p_kpe, p_ret, p_gla, p_k, p_v, p_conv,
            s_ckv, s_kpe, s_ret, s_gla, s_k, s_v, s_conv)
```

```python
import functools

import numpy as np
import jax
import jax.numpy as jnp
from jax import lax
from jax.experimental import pallas as pl
from jax.experimental.pallas import tpu as pltpu

F32 = jnp.float32
BF16 = jnp.bfloat16

D_MODEL = 1024
DEPTH = 2
CHUNK = 64
DN_ALPHA = (2.0 * DEPTH) ** 0.25
LN_EPS = 1e-5
NORM_EPS = 1e-6
ROPE_THETA = 500000.0

MLA_HEADS = 8
MLA_Q_LORA = 256
MLA_KV_LORA = 256
MLA_NOPE = 64
MLA_ROPE = 32
MLA_V = 64
MLA_SCALE = (MLA_NOPE + MLA_ROPE) ** -0.5
MLA_GROUP = 128

RET_HEADS = 8
RET_DK = 64
RET_DV = 64
RET_THETA = 10000.0
RET_LOG_GAMMA = tuple(float(np.log(np.float32(1.0) - np.float32(2.0) ** np.float32(-5.0 - h))) for h in range(RET_HEADS))

GLA_HEADS = 4
GLA_DK = 64
GLA_DV = 128
GLA_LOW_RANK = 16
GLA_GATE_NORM = 16.0
GLA_MAX_EXPONENT = 80.0

SWA_HEADS = 8
SWA_KV_HEADS = 2
SWA_GROUP = SWA_HEADS // SWA_KV_HEADS
SWA_HD = 64
WINDOW = 128
WINDOW_CHUNKS = WINDOW // CHUNK
SWA_ROT = SWA_HD // 4
SWA_SCALE = SWA_HD ** -0.5

D_FF = 2816
CONV_W = 3
FFN_NC = 256
FFN_CHUNKS = D_FF // FFN_NC

EVEN_SIZES = (MLA_Q_LORA, MLA_KV_LORA, MLA_ROPE, RET_HEADS * RET_DK, RET_HEADS * RET_DK, RET_HEADS * RET_DV, RET_HEADS * RET_DV)
ODD_SIZES = (GLA_HEADS * GLA_DK, GLA_HEADS * GLA_DK, GLA_HEADS * GLA_DV, GLA_HEADS * GLA_DV, GLA_LOW_RANK, SWA_HEADS * SWA_HD, SWA_KV_HEADS * SWA_HD, SWA_KV_HEADS * SWA_HD)

LANES = 128
VMEM_LIMIT = 56 * 1024 * 1024
NEG_BIG = -1e30


def _dot(a, b):
    return jnp.dot(a, b, preferred_element_type=F32)


def _dot_nt(a, b):
    return lax.dot_general(a, b, (((1,), (1,)), ((), ())), preferred_element_type=F32)


def _dot_tn(a, b):
    return lax.dot_general(a, b, (((0,), (0,)), ((), ())), preferred_element_type=F32)


def _bf(x):
    return x.astype(BF16)


def _silu(x):
    return x * (1.0 / (1.0 + jnp.exp(-x)))


def _layer_norm(z, g, b):
    mu = jnp.mean(z, axis=-1, keepdims=True)
    zc = z - mu
    var = jnp.mean(zc * zc, axis=-1, keepdims=True)
    return zc * lax.rsqrt(var + LN_EPS) * g + b


def _rms_norm(x, g):
    return x * lax.rsqrt(jnp.mean(x * x, axis=-1, keepdims=True) + NORM_EPS) * g


def _rope(x, c, s_fwd, s_bwd, half):
    width = x.shape[-1]
    fwd = pltpu.roll(x, width - half, 1)
    bwd = pltpu.roll(x, half, 1)
    return x * c + fwd * s_fwd + bwd * s_bwd


def _const_spec(shape, single=False):
    zeros = (0,) * len(shape)
    if single:
        return pl.BlockSpec(shape, lambda *_: zeros, pipeline_mode=pl.Buffered(1))
    return pl.BlockSpec(shape, lambda *_: zeros)


def _params(n_axes):
    return pltpu.CompilerParams(dimension_semantics=("arbitrary",) * n_axes, vmem_limit_bytes=VMEM_LIMIT)


def _rope_tables(pos, dim, theta, group, offset, n_groups, rows):
    inv = 1.0 / (theta ** (jnp.arange(0, dim, 2, dtype=F32) / dim))
    ang = pos.astype(F32)[:, None] * inv[None, :]
    cos, sin = jnp.cos(ang), jnp.sin(ang)
    t = pos.shape[0]
    half = dim // 2
    ones = jnp.ones((t, group), F32)
    zeros = jnp.zeros((t, group), F32)
    c = ones.at[:, offset:offset + dim].set(jnp.concatenate([cos, cos], 1))
    sf = zeros.at[:, offset:offset + half].set(-sin)
    sb = zeros.at[:, offset + half:offset + dim].set(sin)
    reps = rows // t if rows > t else 1
    return tuple(jnp.tile(a, (reps, n_groups)) for a in (c, sf, sb))


def _row_tiling(n_rows, t, tm):
    if t >= tm:
        pb = t // tm
        nb = n_rows // t
        grid = (pb, nb)
        row_map = lambda p, b: (b * pb + p, 0)
        tab_map = lambda p, b: (p, 0)
    else:
        grid = (n_rows // tm, 1)
        row_map = lambda p, b: (p, 0)
        tab_map = lambda p, b: (0, 0)
    return grid, row_map, tab_map


EVEN_W = 2688


def _even_prep_body(prompt, x_ref, w_in_ref, qn_ref, w_uq_ref, kvn_ref, w_uk_ref, w_uv_ref, e_ref,
                    cq_ref, sfq_ref, sbq_ref, ck_ref, sfk_ref, sbk_ref, cr_ref, sfr_ref, sbr_ref,
                    q_out, ckv_out, kpe_out, rq_out, rk_out, rv_out, rg_out, *kv_outs):
    h = _dot(_bf(x_ref[...]), w_in_ref[...])
    q_lat = _rms_norm(h[:, 0:256], qn_ref[...])
    q = _dot(_bf(q_lat), w_uq_ref[...])
    q_out[...] = _bf(_rope(q, cq_ref[...], sfq_ref[...], sbq_ref[...], MLA_ROPE // 2))
    ckv = _rms_norm(h[:, 256:512], kvn_ref[...])
    ckv_out[...] = ckv
    kslab = _rope(h[:, 512:640], ck_ref[...], sfk_ref[...], sbk_ref[...], MLA_ROPE // 2)
    kpe_out[...] = kslab[:, :MLA_ROPE]
    rq = h[:, 640:1152]
    rk = h[:, 1152:1664]
    cr, sfr, sbr = cr_ref[...], sfr_ref[...], sbr_ref[...]
    rq_out[...] = _bf(_rope(rq, cr, sfr, sbr, RET_DK // 2))
    rk_out[...] = _bf(_rope(rk, cr, sfr, sbr, RET_DK // 2) * (RET_DK ** -0.5))
    rv_out[...] = _bf(h[:, 1664:2176])
    rg_out[...] = h[:, 2176:2688]
    if prompt:
        k_out, v_out = kv_outs
        ckv_b = _bf(ckv)
        k_out[...] = _bf(_dot(ckv_b, w_uk_ref[...]) + _dot(_bf(kslab), e_ref[...]))
        v_out[...] = _bf(_dot(ckv_b, w_uv_ref[...]))


def _even_prep(x, pos, t, wts, prompt, tm):
    n = x.shape[0]
    grid, row_map, tab_map = _row_tiling(n, t, tm)
    rows = max(t, tm)
    tq = _rope_tables(pos, MLA_ROPE, ROPE_THETA, MLA_GROUP, MLA_NOPE, MLA_HEADS, rows)
    tk = _rope_tables(pos, MLA_ROPE, ROPE_THETA, LANES, 0, 1, rows)
    tr = _rope_tables(pos, RET_DK, RET_THETA, RET_DK, 0, RET_HEADS, rows)

    def row(width):
        return pl.BlockSpec((tm, width), row_map)

    def tab(width):
        return pl.BlockSpec((tm, width), tab_map)

    in_specs = [row(D_MODEL), _const_spec((D_MODEL, EVEN_W)), _const_spec((1, 256)), _const_spec((256, 1024)),
                _const_spec((1, 256)), _const_spec((256, 1024)), _const_spec((256, 1024)), _const_spec((LANES, 1024)),
                tab(1024), tab(1024), tab(1024), tab(LANES), tab(LANES), tab(LANES), tab(512), tab(512), tab(512)]
    out_shape = [jax.ShapeDtypeStruct((n, 1024), BF16), jax.ShapeDtypeStruct((n, 256), F32),
                 jax.ShapeDtypeStruct((n, MLA_ROPE), F32), jax.ShapeDtypeStruct((n, 512), BF16),
                 jax.ShapeDtypeStruct((n, 512), BF16), jax.ShapeDtypeStruct((n, 512), BF16),
                 jax.ShapeDtypeStruct((n, 512), F32)]
    out_specs = [row(1024), row(256), row(MLA_ROPE), row(512), row(512), row(512), row(512)]
    if prompt:
        out_shape += [jax.ShapeDtypeStruct((n, 1024), BF16), jax.ShapeDtypeStruct((n, 1024), BF16)]
        out_specs += [row(1024), row(1024)]
    return pl.pallas_call(
        functools.partial(_even_prep_body, prompt),
        grid=grid, in_specs=in_specs, out_specs=out_specs, out_shape=out_shape,
        compiler_params=_params(2), name="even_prep_prompt" if prompt else "even_prep_sample",
    )(x, wts["w_in"], wts["q_norm"], wts["w_uq"], wts["kv_norm"], wts["w_uk"], wts["w_uv"], wts["e_kpe"],
      *tq, *tk, *tr)


MLA_TQ = 256


def _mla_prompt_body(q_ref, k_ref, v_ref, o_ref):
    i = pl.program_id(1)
    tq = MLA_TQ
    row_chunk = lax.broadcasted_iota(jnp.int32, (tq, tq), 0) // CHUNK
    col_chunk = lax.broadcasted_iota(jnp.int32, (tq, tq), 1) // CHUNK
    visible = col_chunk <= row_chunk
    outs = []
    for h in range(MLA_HEADS):
        lo = h * MLA_GROUP
        q = q_ref[:, lo:lo + MLA_GROUP]

        def block(kb, carry, masked, q=q, lo=lo):
            m, l, acc = carry
            start = pl.multiple_of(kb * tq, tq)
            k = k_ref[pl.ds(start, tq), lo:lo + MLA_GROUP]
            v = v_ref[pl.ds(start, tq), lo:lo + MLA_GROUP]
            s = _dot_nt(q, k)
            if masked:
                s = jnp.where(visible, s, NEG_BIG)
            m_new = jnp.maximum(m, jnp.max(s, axis=1, keepdims=True))
            alpha = jnp.exp((m - m_new) * MLA_SCALE)
            p = jnp.exp((s - m_new) * MLA_SCALE)
            l = alpha * l + jnp.sum(p, axis=1, keepdims=True)
            acc = alpha * acc + _dot(_bf(p), v)
            return m_new, l, acc

        init = (jnp.full((tq, 1), NEG_BIG, F32), jnp.zeros((tq, 1), F32), jnp.zeros((tq, MLA_GROUP), F32))
        carry = lax.fori_loop(0, i, lambda kb, c: block(kb, c, False), init)
        _, l, acc = block(i, carry, True)
        outs.append((acc * (1.0 / l))[:, :MLA_V])
    o_ref[...] = _bf(jnp.concatenate(outs, axis=1))


def _mla_prompt(q, k, v, n_streams, t):
    nq = t // MLA_TQ
    return pl.pallas_call(
        _mla_prompt_body,
        grid=(n_streams, nq),
        in_specs=[pl.BlockSpec((MLA_TQ, 1024), lambda b, i: (b * nq + i, 0)),
                  pl.BlockSpec((t, 1024), lambda b, i: (b, 0)),
                  pl.BlockSpec((t, 1024), lambda b, i: (b, 0))],
        out_specs=pl.BlockSpec((MLA_TQ, 512), lambda b, i: (b * nq + i, 0)),
        out_shape=jax.ShapeDtypeStruct((n_streams * t, 512), BF16),
        compiler_params=_params(2), name="mla_prompt_attention",
    )(q, k, v)


def _mla_sample_body(q_ref, ckv_past_ref, kpe_past_ref, ckv_new_ref, kpe_new_ref, wk_ref, wv_ref, o_ref):
    t = q_ref.shape[0]
    q = q_ref[...]
    qa, qp = [], []
    for h in range(MLA_HEADS):
        lo = h * MLA_GROUP
        qa.append(_dot(q[:, lo:lo + MLA_NOPE], wk_ref[h]))
        qp.append(q[:, lo + MLA_NOPE:lo + MLA_NOPE + MLA_ROPE])
    qa = _bf(jnp.concatenate(qa, axis=0))
    qp = jnp.concatenate(qp, axis=0)
    ckv_p = _bf(ckv_past_ref[0])
    kpe_p = _bf(kpe_past_ref[0])
    ckv_n = _bf(ckv_new_ref[...])
    kpe_n = _bf(kpe_new_ref[...])
    s_p = _dot_nt(qa, ckv_p) + _dot_nt(qp, kpe_p)
    s_n = _dot_nt(qa, ckv_n) + _dot_nt(qp, kpe_n)
    m = jnp.maximum(jnp.max(s_p, axis=1, keepdims=True), jnp.max(s_n, axis=1, keepdims=True))
    p_p = jnp.exp((s_p - m) * MLA_SCALE)
    p_n = jnp.exp((s_n - m) * MLA_SCALE)
    inv = 1.0 / (jnp.sum(p_p, axis=1, keepdims=True) + jnp.sum(p_n, axis=1, keepdims=True))
    o_lat = _bf((_dot(_bf(p_p), ckv_p) + _dot(_bf(p_n), ckv_n)) * inv)
    outs = [_dot(o_lat[h * t:(h + 1) * t], wv_ref[h]) for h in range(MLA_HEADS)]
    o_ref[...] = _bf(jnp.concatenate(outs, axis=1))


def _mla_sample(q, ckv_past, kpe_past, ckv_new, kpe_new, wk_t, wv, n_streams, t):
    past = ckv_past.shape[1]
    return pl.pallas_call(
        _mla_sample_body,
        grid=(n_streams,),
        in_specs=[pl.BlockSpec((t, 1024), lambda b: (b, 0)),
                  pl.BlockSpec((1, past, MLA_KV_LORA), lambda b: (b, 0, 0)),
                  pl.BlockSpec((1, past, MLA_ROPE), lambda b: (b, 0, 0)),
                  pl.BlockSpec((t, MLA_KV_LORA), lambda b: (b, 0)),
                  pl.BlockSpec((t, MLA_ROPE), lambda b: (b, 0)),
                  _const_spec((MLA_HEADS, MLA_NOPE, MLA_KV_LORA)),
                  _const_spec((MLA_HEADS, MLA_KV_LORA, MLA_V))],
        out_specs=pl.BlockSpec((t, 512), lambda b: (b, 0)),
        out_shape=jax.ShapeDtypeStruct((n_streams * t, 512), BF16),
        compiler_params=_params(1), name="mla_sample_attention",
    )(q, ckv_past, kpe_past, ckv_new, kpe_new, wk_t, wv)


def _retention_body(chunk, n_chunks, q_ref, k_ref, v_ref, g_ref, ng_ref, s0_ref, o_ref, s_out_ref, s_scr, d_scr):
    b = pl.program_id(0)
    c = pl.program_id(1)

    @pl.when(jnp.logical_and(b == 0, c == 0))
    def _():
        n_idx = lax.broadcasted_iota(jnp.int32, (chunk, chunk), 0)
        m_idx = lax.broadcasted_iota(jnp.int32, (chunk, chunk), 1)
        diff = (n_idx - m_idx).astype(F32)
        for h in range(RET_HEADS):
            d_scr[h] = jnp.where(n_idx >= m_idx, jnp.exp(diff * RET_LOG_GAMMA[h]), 0.0)

    @pl.when(c == 0)
    def _():
        s_scr[...] = s0_ref[0]

    n_col = lax.broadcasted_iota(jnp.int32, (chunk, 1), 0).astype(F32)
    outs = []
    for h in range(RET_HEADS):
        lo = h * RET_DK
        q = q_ref[:, lo:lo + RET_DK]
        k = k_ref[:, lo:lo + RET_DK]
        v = v_ref[:, lo:lo + RET_DV]
        lg = RET_LOG_GAMMA[h]
        s_prev = s_scr[h]
        a = _dot_nt(q, k) * d_scr[h]
        o = _dot(_bf(a), v) + jnp.exp((n_col + 1.0) * lg) * _dot(q, _bf(s_prev))
        k_dec = _bf(k.astype(F32) * jnp.exp((chunk - 1.0 - n_col) * lg))
        s_scr[h] = float(np.exp(chunk * lg)) * s_prev + _dot_tn(k_dec, v)
        oc = o - jnp.mean(o, axis=-1, keepdims=True)
        y = oc * lax.rsqrt(jnp.mean(oc * oc, axis=-1, keepdims=True) + NORM_EPS)
        outs.append(y)
    y = jnp.concatenate(outs, axis=1) * ng_ref[...]
    o_ref[...] = _bf(y * _silu(g_ref[...]))

    @pl.when(c == n_chunks - 1)
    def _():
        s_out_ref[0] = s_scr[...]


def _retention(rq, rk, rv, rg, norm_g, s0, n_streams, t, chunk):
    nc = t // chunk
    row = lambda b, c: (b * nc + c, 0)
    blk = lambda dt: pl.BlockSpec((chunk, 512), row)
    state_spec = pl.BlockSpec((1, RET_HEADS, RET_DK, RET_DV), lambda b, c: (b, 0, 0, 0))
    return pl.pallas_call(
        functools.partial(_retention_body, chunk, nc),
        grid=(n_streams, nc),
        in_specs=[blk(BF16), blk(BF16), blk(BF16), blk(F32), _const_spec((1, 512)), state_spec],
        out_specs=[pl.BlockSpec((chunk, 512), row), state_spec],
        out_shape=[jax.ShapeDtypeStruct((n_streams * t, 512), BF16),
                   jax.ShapeDtypeStruct((n_streams, RET_HEADS, RET_DK, RET_DV), F32)],
        scratch_shapes=[pltpu.VMEM((RET_HEADS, RET_DK, RET_DV), F32), pltpu.VMEM((RET_HEADS, chunk, chunk), F32)],
        compiler_params=_params(2), name="retention",
    )(rq, rk, rv, rg, norm_g, s0)


def _out_proj_body(a_ref, r_ref, x_ref, wa_ref, wr_ref, g_ref, b_ref, o_ref):
    m = _dot(a_ref[...], wa_ref[...]) + _dot(r_ref[...], wr_ref[...])
    o_ref[...] = _layer_norm(DN_ALPHA * x_ref[...] + m, g_ref[...], b_ref[...])


def _out_proj(a, r, x, w_a, w_r, g, b, tm):
    n = x.shape[0]
    row = lambda w: pl.BlockSpec((tm, w), lambda i: (i, 0))
    return pl.pallas_call(
        _out_proj_body,
        grid=(n // tm,),
        in_specs=[row(512), row(512), row(D_MODEL), _const_spec((512, D_MODEL)), _const_spec((512, D_MODEL)),
                  _const_spec((1, D_MODEL)), _const_spec((1, D_MODEL))],
        out_specs=row(D_MODEL),
        out_shape=jax.ShapeDtypeStruct((n, D_MODEL), F32),
        compiler_params=_params(1), name="out_proj_deepnorm",
    )(a, r, x, w_a, w_r, g, b)


def _ffn_body(has_state, tt, x_ref, w_up_ref, cw_ref, cb_ref, w_dn_ref, g_ref, b_ref, *rest):
    if has_state:
        st_ref, o_ref, conv_ref, acc_ref = rest
    else:
        o_ref, conv_ref, acc_ref = rest
        st_ref = conv_ref

        @pl.when(pl.program_id(1) == 0)
        def _():
            conv_ref[...] = jnp.zeros_like(conv_ref)

    tm = x_ref.shape[0]
    nb = tm // tt
    x = x_ref[...]
    xb = _bf(x)
    t_idx = lax.broadcasted_iota(jnp.int32, (nb, tt, FFN_NC), 1)
    acc_ref[...] = jnp.zeros_like(acc_ref)

    def chunk_step(c, _):
        halves = []
        for half in range(2):
            u = _dot(xb, w_up_ref[half, c]).reshape(nb, tt, FFN_NC)
            prev = st_ref[half, c]
            conv_ref[half, c] = u[:, tt - 2:tt, :]
            p0 = prev[:, 0:1, :]
            p1 = prev[:, 1:2, :]
            u1 = jnp.where(t_idx == 0, p1, pltpu.roll(u, 1, 1))
            u2 = jnp.where(t_idx == 0, p0, jnp.where(t_idx == 1, p1, pltpu.roll(u, 2, 1)))
            w = cw_ref[half, c]
            cv = cb_ref[half, c] + u2 * w[0:1, :] + u1 * w[1:2, :] + u * w[2:3, :]
            halves.append(cv.reshape(tm, FFN_NC))
        gated = _bf(_silu(halves[0]) * halves[1])
        acc_ref[...] += _dot(gated, w_dn_ref[c])
        return 0

    lax.fori_loop(0, FFN_CHUNKS, chunk_step, 0)
    o_ref[...] = _layer_norm(DN_ALPHA * x + acc_ref[...], g_ref[...], b_ref[...])


def _ffn(x, t, wts, g, b, state, tm):
    n = x.shape[0]
    n_streams = n // t
    has_state = state is not None
    if t >= tm:
        tt, pb = tm, t // tm
        grid = (n_streams, pb)
        row_map = lambda s, p: (s * pb + p, 0)
        st_map = lambda s, p: (0, 0, s, 0, 0)
        nb = 1
    else:
        tt, nb = t, tm // t
        grid = (n // tm, 1)
        row_map = lambda s, p: (s, 0)
        st_map = lambda s, p: (0, 0, s, 0, 0)
    st_spec = pl.BlockSpec((2, FFN_CHUNKS, nb, CONV_W - 1, FFN_NC), st_map)
    y, conv = pl.pallas_call(
        functools.partial(_ffn_body, has_state, tt),
        grid=grid,
        in_specs=[pl.BlockSpec((tm, D_MODEL), row_map),
                  _const_spec((2, FFN_CHUNKS, D_MODEL, FFN_NC), single=True),
                  _const_spec((2, FFN_CHUNKS, CONV_W, FFN_NC)),
                  _const_spec((2, FFN_CHUNKS, 1, FFN_NC)),
                  _const_spec((FFN_CHUNKS, FFN_NC, D_MODEL), single=True),
                  _const_spec((1, D_MODEL)), _const_spec((1, D_MODEL))] + ([st_spec] if has_state else []),
        out_specs=[pl.BlockSpec((tm, D_MODEL), row_map), st_spec],
        out_shape=[jax.ShapeDtypeStruct((n, D_MODEL), F32),
                   jax.ShapeDtypeStruct((2, FFN_CHUNKS, n_streams, CONV_W - 1, FFN_NC), F32)],
        scratch_shapes=[pltpu.VMEM((tm, D_MODEL), F32)],
        compiler_params=_params(2), name="conv_ffn_deepnorm",
    )(x, wts["w_up"], wts["conv_w"], wts["conv_b"], wts["w_dn"], g, b, *([state] if has_state else []))
    return y, conv


def _conv_rows_to_chunks(st):
    s = st.shape[0]
    return st.reshape(s, CONV_W - 1, 2, FFN_CHUNKS, FFN_NC).transpose(2, 3, 0, 1, 4)


def _conv_chunks_to_rows(c):
    s = c.shape[2]
    return c.transpose(2, 3, 0, 1, 4).reshape(s, CONV_W - 1, 2 * D_FF)


ODD_W = 2432


def _odd_prep_body(x_ref, w_in_ref, w_a2_ref, b_a_ref, cq_ref, sfq_ref, sbq_ref, ck_ref, sfk_ref, sbk_ref,
                   gq_out, gk_out, gv_out, gg_out, gdec_out, sq_out, sk_out, sv_out):
    h = _dot(_bf(x_ref[...]), w_in_ref[...])
    gq_out[...] = h[:, 0:256] * (GLA_DK ** -0.5)
    gk_out[...] = h[:, 256:512]
    gv_out[...] = _bf(h[:, 512:1024])
    gg_out[...] = h[:, 1024:1536]
    logit = _dot(_bf(h[:, 1536:1664]), w_a2_ref[...]) + b_a_ref[...]
    log_sig = jnp.minimum(logit, 0.0) - jnp.log(1.0 + jnp.exp(-jnp.abs(logit)))
    gdec_out[...] = log_sig * (1.0 / GLA_GATE_NORM)
    sq_out[...] = _bf(_rope(h[:, 1664:2176], cq_ref[...], sfq_ref[...], sbq_ref[...], SWA_ROT // 2))
    sk_out[...] = _rope(h[:, 2176:2304], ck_ref[...], sfk_ref[...], sbk_ref[...], SWA_ROT // 2)
    sv_out[...] = h[:, 2304:2432]


def _odd_prep(x, pos, t, wts, tm):
    n = x.shape[0]
    grid, row_map, tab_map = _row_tiling(n, t, tm)
    rows = max(t, tm)
    tq = _rope_tables(pos, SWA_ROT, ROPE_THETA, SWA_HD, 0, SWA_HEADS, rows)
    tk = _rope_tables(pos, SWA_ROT, ROPE_THETA, SWA_HD, 0, SWA_KV_HEADS, rows)
    row = lambda w: pl.BlockSpec((tm, w), row_map)
    tab = lambda w: pl.BlockSpec((tm, w), tab_map)
    return pl.pallas_call(
        _odd_prep_body,
        grid=grid,
        in_specs=[row(D_MODEL), _const_spec((D_MODEL, ODD_W)), _const_spec((LANES, 256)), _const_spec((1, 256)),
                  tab(512), tab(512), tab(512), tab(128), tab(128), tab(128)],
        out_specs=[row(256), row(256), row(512), row(512), row(256), row(512), row(128), row(128)],
        out_shape=[jax.ShapeDtypeStruct((n, 256), F32), jax.ShapeDtypeStruct((n, 256), F32),
                   jax.ShapeDtypeStruct((n, 512), BF16), jax.ShapeDtypeStruct((n, 512), F32),
                   jax.ShapeDtypeStruct((n, 256), F32), jax.ShapeDtypeStruct((n, 512), BF16),
                   jax.ShapeDtypeStruct((n, 128), F32), jax.ShapeDtypeStruct((n, 128), F32)],
        compiler_params=_params(2), name="odd_prep",
    )(x, wts["w_in"], wts["w_a2"], wts["b_a"], *tq, *tk)


def _gla_body(chunk, n_sub, n_blocks, q_ref, k_ref, v_ref, gg_ref, gd_ref, ng_ref, s0_ref, o_ref, s_out_ref, s_scr):
    blk = pl.program_id(1)

    @pl.when(blk == 0)
    def _():
        s_scr[...] = s0_ref[0]

    n_idx = lax.broadcasted_iota(jnp.int32, (chunk, chunk), 0)
    m_idx = lax.broadcasted_iota(jnp.int32, (chunk, chunk), 1)
    causal = n_idx >= m_idx
    r_idx = lax.broadcasted_iota(jnp.int32, (chunk, GLA_HEADS * GLA_DK), 0)

    def sub_chunk(j, _):
        start = pl.multiple_of(j * chunk, chunk)
        rows = pl.ds(start, chunk)
        b = gd_ref[rows, :]
        step = 1
        while step < chunk:
            b = b + jnp.where(r_idx >= step, pltpu.roll(b, step, 0), 0.0)
            step *= 2
        b_last = b[chunk - 1:chunk, :]
        q = q_ref[rows, :]
        k = k_ref[rows, :]
        q_dec = _bf(q * jnp.exp(b))
        k_inv = _bf(k * jnp.exp(jnp.minimum(-b, GLA_MAX_EXPONENT)))
        k_dec = _bf(k * jnp.exp(b_last - b))
        s_decay = jnp.exp(b_last)
        v_all = v_ref[rows, :]
        outs = []
        for h in range(GLA_HEADS):
            lo = h * GLA_DK
            qh = q_dec[:, lo:lo + GLA_DK]
            vh = v_all[:, h * GLA_DV:(h + 1) * GLA_DV]
            s_prev = s_scr[h]
            a = jnp.where(causal, _dot_nt(qh, k_inv[:, lo:lo + GLA_DK]), 0.0)
            o = _dot(_bf(a), vh) + _dot_nt(qh, _bf(s_prev))
            s_scr[h] = s_prev * s_decay[:, lo:lo + GLA_DK] + _dot_tn(vh, k_dec[:, lo:lo + GLA_DK])
            outs.append(o * lax.rsqrt(jnp.mean(o * o, axis=-1, keepdims=True) + NORM_EPS))
        y = jnp.concatenate(outs, axis=1) * ng_ref[...]
        o_ref[rows, :] = _bf(y * _silu(gg_ref[rows, :]))
        return 0

    lax.fori_loop(0, n_sub, sub_chunk, 0)

    @pl.when(blk == n_blocks - 1)
    def _():
        s_out_ref[0] = s_scr[...]


def _gla(gq, gk, gv, gg, gdec, norm_g, s0_t, n_streams, t, chunk, rows):
    nblk = t // rows
    row = lambda b, c: (b * nblk + c, 0)
    blk = lambda w: pl.BlockSpec((rows, w), row)
    state_spec = pl.BlockSpec((1, GLA_HEADS, GLA_DV, GLA_DK), lambda b, c: (b, 0, 0, 0))
    return pl.pallas_call(
        functools.partial(_gla_body, chunk, rows // chunk, nblk),
        grid=(n_streams, nblk),
        in_specs=[blk(256), blk(256), blk(512), blk(512), blk(256), _const_spec((1, 512)), state_spec],
        out_specs=[blk(512), state_spec],
        out_shape=[jax.ShapeDtypeStruct((n_streams * t, 512), BF16),
                   jax.ShapeDtypeStruct((n_streams, GLA_HEADS, GLA_DV, GLA_DK), F32)],
        scratch_shapes=[pltpu.VMEM((GLA_HEADS, GLA_DV, GLA_DK), F32)],
        compiler_params=_params(2), name="gated_linear_attention",
    )(gq, gk, gv, gg, gdec, norm_g, s0_t)


def _sink_softmax(scores, sink):
    m = sink
    for s in scores:
        m = jnp.maximum(m, jnp.max(s, axis=1, keepdims=True))
    ps = [jnp.exp(s - m) for s in scores]
    denom = jnp.exp(sink - m)
    for p in ps:
        denom = denom + jnp.sum(p, axis=1, keepdims=True)
    inv = 1.0 / denom
    return [_bf(p * inv) for p in ps]


def _sink_column(sinks_ref, kvh, rows_per_head):
    return jnp.concatenate([jnp.full((rows_per_head, 1), sinks_ref[kvh * SWA_GROUP + g], F32)
                            for g in range(SWA_GROUP)], axis=0)


def _swa_prompt_body(n_chunks, sinks_ref, q_ref, k_ref, v_ref, o_ref):
    nk = (WINDOW_CHUNKS + 1) * CHUNK
    col_chunk = lax.broadcasted_iota(jnp.int32, (SWA_GROUP * CHUNK, nk), 1) // CHUNK
    sink_cols = [_sink_column(sinks_ref, kvh, CHUNK) for kvh in range(SWA_KV_HEADS)]

    def chunk_step(c, _):
        first = jnp.maximum(c - WINDOW_CHUNKS, 0)
        kstart = pl.multiple_of(first * CHUNK, CHUNK)
        qstart = pl.multiple_of(c * CHUNK, CHUNK)
        kb = _bf(k_ref[pl.ds(kstart, nk), :])
        vb = _bf(v_ref[pl.ds(kstart, nk), :])
        qc = q_ref[pl.ds(qstart, CHUNK), :]
        visible = (col_chunk + first) <= c
        pieces = []
        for kvh in range(SWA_KV_HEADS):
            qs = jnp.concatenate([qc[:, (kvh * SWA_GROUP + g) * SWA_HD:(kvh * SWA_GROUP + g + 1) * SWA_HD]
                                  for g in range(SWA_GROUP)], axis=0)
            s = _dot_nt(qs, kb[:, kvh * SWA_HD:(kvh + 1) * SWA_HD]) * SWA_SCALE
            s = jnp.where(visible, s, NEG_BIG)
            (p,) = _sink_softmax([s], sink_cols[kvh])
            o = _dot(p, vb[:, kvh * SWA_HD:(kvh + 1) * SWA_HD])
            pieces += [o[g * CHUNK:(g + 1) * CHUNK] for g in range(SWA_GROUP)]
        o_ref[pl.ds(qstart, CHUNK), :] = _bf(jnp.concatenate(pieces, axis=1))
        return 0

    lax.fori_loop(0, n_chunks, chunk_step, 0)


def _swa_prompt(sinks, q, k, v, n_streams, t):
    blk = lambda w: pl.BlockSpec((t, w), lambda b: (b, 0))
    return pl.pallas_call(
        functools.partial(_swa_prompt_body, t // CHUNK),
        grid=(n_streams,),
        in_specs=[pl.BlockSpec(memory_space=pltpu.SMEM), blk(512), blk(128), blk(128)],
        out_specs=blk(512),
        out_shape=jax.ShapeDtypeStruct((n_streams * t, 512), BF16),
        compiler_params=_params(1), name="swa_prompt_attention",
    )(sinks, q, k, v)


def _swa_sample_body(sinks_ref, q_ref, kn_ref, vn_ref, kc_ref, vc_ref, o_ref):
    t = q_ref.shape[0]
    q = q_ref[...]
    kn, vn = _bf(kn_ref[...]), _bf(vn_ref[...])
    kc, vc = _bf(kc_ref[0]), _bf(vc_ref[0])
    pieces = []
    for kvh in range(SWA_KV_HEADS):
        lo = kvh * SWA_HD
        qs = jnp.concatenate([q[:, (kvh * SWA_GROUP + g) * SWA_HD:(kvh * SWA_GROUP + g + 1) * SWA_HD]
                              for g in range(SWA_GROUP)], axis=0)
        s_c = _dot_nt(qs, kc[:, lo:lo + SWA_HD]) * SWA_SCALE
        s_n = _dot_nt(qs, kn[:, lo:lo + SWA_HD]) * SWA_SCALE
        p_c, p_n = _sink_softmax([s_c, s_n], _sink_column(sinks_ref, kvh, t))
        o = _dot(p_c, vc[:, lo:lo + SWA_HD]) + _dot(p_n, vn[:, lo:lo + SWA_HD])
        pieces += [o[g * t:(g + 1) * t] for g in range(SWA_GROUP)]
    o_ref[...] = _bf(jnp.concatenate(pieces, axis=1))


def _swa_sample(sinks, q, k_new, v_new, k_cache, v_cache, n_streams, t):
    rows = k_cache.shape[1]
    new = lambda w: pl.BlockSpec((t, w), lambda b: (b, 0))
    cache = pl.BlockSpec((1, rows, 128), lambda b: (b, 0, 0))
    return pl.pallas_call(
        _swa_sample_body,
        grid=(n_streams,),
        in_specs=[pl.BlockSpec(memory_space=pltpu.SMEM), new(512), new(128), new(128), cache, cache],
        out_specs=new(512),
        out_shape=jax.ShapeDtypeStruct((n_streams * t, 512), BF16),
        compiler_params=_params(1), name="swa_sample_attention",
    )(sinks, q, k_new, v_new, k_cache, v_cache)


def _even_weights(w_in, q_norm, w_uq, kv_norm, w_ukv, ret_norm, w_out):
    segs = jnp.split(w_in, np.cumsum(EVEN_SIZES)[:-1].tolist(), axis=1)
    pad = jnp.zeros((D_MODEL, LANES - MLA_ROPE), F32)
    w_in_p = jnp.concatenate(segs[:3] + [pad] + segs[3:], axis=1)
    hd = MLA_NOPE + MLA_ROPE
    w_uq_p = jnp.pad(w_uq.reshape(MLA_Q_LORA, MLA_HEADS, hd), ((0, 0), (0, 0), (0, MLA_GROUP - hd)))
    w3 = w_ukv.reshape(MLA_KV_LORA, MLA_HEADS, MLA_NOPE + MLA_V)
    w_uk, w_uv = w3[..., :MLA_NOPE], w3[..., MLA_NOPE:]
    pad_k = ((0, 0), (0, 0), (0, MLA_GROUP - MLA_NOPE))
    e = np.zeros((LANES, MLA_HEADS, MLA_GROUP), np.float32)
    for r in range(MLA_ROPE):
        e[r, :, MLA_NOPE + r] = 1.0
    return {
        "w_in": _bf(w_in_p),
        "q_norm": q_norm.reshape(1, -1), "kv_norm": kv_norm.reshape(1, -1),
        "w_uq": _bf(w_uq_p.reshape(MLA_Q_LORA, -1)),
        "w_uk": _bf(jnp.pad(w_uk, pad_k).reshape(MLA_KV_LORA, -1)),
        "w_uv": _bf(jnp.pad(w_uv, pad_k).reshape(MLA_KV_LORA, -1)),
        "e_kpe": jnp.asarray(e.reshape(LANES, -1), BF16),
        "w_uk_t": _bf(w_uk.transpose(1, 2, 0)),
        "w_uv_h": _bf(w_uv.transpose(1, 0, 2)),
        "ret_norm": ret_norm.reshape(1, -1),
        "w_out_a": _bf(w_out[:MLA_HEADS * MLA_V]), "w_out_r": _bf(w_out[MLA_HEADS * MLA_V:]),
    }


def _odd_weights(w_in, w_a2, b_a, gla_norm, sinks, w_out):
    segs = jnp.split(w_in, np.cumsum(ODD_SIZES)[:-1].tolist(), axis=1)
    pad = jnp.zeros((D_MODEL, LANES - GLA_LOW_RANK), F32)
    w_in_p = jnp.concatenate(segs[:5] + [pad] + segs[5:], axis=1)
    w_a2_p = jnp.pad(w_a2, ((0, LANES - GLA_LOW_RANK), (0, 0)))
    return {
        "w_in": _bf(w_in_p), "w_a2": _bf(w_a2_p), "b_a": b_a.reshape(1, -1),
        "gla_norm": gla_norm.reshape(1, -1), "sinks": sinks,
        "w_out_a": _bf(w_out[:GLA_HEADS * GLA_DV]), "w_out_r": _bf(w_out[GLA_HEADS * GLA_DV:]),
    }


def _ffn_weights(w_up, conv_w, conv_b, w_down):
    return {
        "w_up": _bf(w_up.reshape(D_MODEL, 2, FFN_CHUNKS, FFN_NC).transpose(1, 2, 0, 3)),
        "conv_w": conv_w.reshape(CONV_W, 2, FFN_CHUNKS, FFN_NC).transpose(1, 2, 0, 3),
        "conv_b": conv_b.reshape(2, FFN_CHUNKS, 1, FFN_NC),
        "w_dn": _bf(w_down.reshape(FFN_CHUNKS, FFN_NC, D_MODEL)),
    }


def _run_trunk(x3, pos, st, wts, ln, prompt):
    n_streams, t, _ = x3.shape
    n = n_streams * t
    x = x3.reshape(n, D_MODEL)
    tm = 512 if n >= 512 else n
    we, wo = wts["even"], wts["odd"]

    prep = _even_prep(x, pos, t, we, prompt, min(tm, 256))
    q, ckv, kpe, rq, rk, rv, rg = prep[:7]
    if prompt:
        a = _mla_prompt(q, prep[7], prep[8], n_streams, t)
        ret_s0 = jnp.zeros((n_streams, RET_HEADS, RET_DK, RET_DV), F32)
        ret_chunk = 256
    else:
        a = _mla_sample(q, st["ckv"], st["kpe"], ckv, kpe, we["w_uk_t"], we["w_uv_h"], n_streams, t)
        ret_s0 = st["ret"]
        ret_chunk = t
    r, ret_s = _retention(rq, rk, rv, rg, we["ret_norm"], ret_s0, n_streams, t, ret_chunk)
    x = _out_proj(a, r, x, we["w_out_a"], we["w_out_r"], ln["mix_g"][0], ln["mix_b"][0], tm)
    x, conv0 = _ffn(x, t, wts["ffn"][0], ln["ffn_g"][0], ln["ffn_b"][0], None if prompt else st["conv"][0], tm)

    gq, gk, gv, gg, gdec, sq, sk, sv = _odd_prep(x, pos, t, wo, min(tm, 256))
    if prompt:
        gla_s0 = jnp.zeros((n_streams, GLA_HEADS, GLA_DV, GLA_DK), F32)
        y_gla, gla_s = _gla(gq, gk, gv, gg, gdec, wo["gla_norm"], gla_s0, n_streams, t, CHUNK, 512)
        a = _swa_prompt(wo["sinks"], sq, sk, sv, n_streams, t)
        k_buf = sk.reshape(n_streams, t, SWA_KV_HEADS, SWA_HD)[:, -WINDOW:]
        v_buf = sv.reshape(n_streams, t, SWA_KV_HEADS, SWA_HD)[:, -WINDOW:]
    else:
        y_gla, gla_s = _gla(gq, gk, gv, gg, gdec, wo["gla_norm"], st["gla_t"], n_streams, t, t, t)
        rows = st["swa_k"].shape[1]
        kc = st["swa_k"].reshape(n_streams, rows, SWA_KV_HEADS * SWA_HD)
        vc = st["swa_v"].reshape(n_streams, rows, SWA_KV_HEADS * SWA_HD)
        a = _swa_sample(wo["sinks"], sq, sk, sv, kc, vc, n_streams, t)
        k_all = jnp.concatenate([kc, sk.reshape(n_streams, t, -1)], axis=1)[:, -rows:]
        v_all = jnp.concatenate([vc, sv.reshape(n_streams, t, -1)], axis=1)[:, -rows:]
        k_buf = k_all.reshape(n_streams, rows, SWA_KV_HEADS, SWA_HD)
        v_buf = v_all.reshape(n_streams, rows, SWA_KV_HEADS, SWA_HD)
    x = _out_proj(y_gla, a, x, wo["w_out_a"], wo["w_out_r"], ln["mix_g"][1], ln["mix_b"][1], tm)
    x, conv1 = _ffn(x, t, wts["ffn"][1], ln["ffn_g"][1], ln["ffn_b"][1], None if prompt else st["conv"][1], tm)

    return (x.reshape(n_streams, t, D_MODEL),
            ckv.reshape(1, n_streams, t, MLA_KV_LORA), kpe.reshape(1, n_streams, t, MLA_ROPE),
            ret_s[None], gla_s.swapaxes(-1, -2)[None], k_buf[None], v_buf[None],
            jnp.stack([_conv_chunks_to_rows(conv0), _conv_chunks_to_rows(conv1)]))


def kernel(x_prompt, x_sample, cache_mla_ckv, cache_mla_kpe, state_ret, state_gla, cache_swa_k, cache_swa_v, state_ffn_conv, w_in_even, mla_q_norm, mla_w_uq, mla_kv_norm, mla_w_ukv, ret_norm, w_out_even, w_in_odd, gla_w_a2, gla_b_a, gla_norm, swa_sinks, w_out_odd, ffn_w_up, ffn_conv_w, ffn_conv_b, ffn_w_down, ln_mix_g, ln_mix_b, ln_ffn_g, ln_ffn_b):
    assert DEPTH == 2 and w_in_even.shape[0] == 1 and w_in_odd.shape[0] == 1
    wts = {
        "even": _even_weights(w_in_even[0], mla_q_norm[0], mla_w_uq[0], mla_kv_norm[0], mla_w_ukv[0], ret_norm[0], w_out_even[0]),
        "odd": _odd_weights(w_in_odd[0], gla_w_a2[0], gla_b_a[0], gla_norm[0], swa_sinks[0], w_out_odd[0]),
        "ffn": [_ffn_weights(ffn_w_up[l], ffn_conv_w[l], ffn_conv_b[l], ffn_w_down[l]) for l in range(DEPTH)],
    }
    ln = {"mix_g": ln_mix_g[:, None, :], "mix_b": ln_mix_b[:, None, :],
          "ffn_g": ln_ffn_g[:, None, :], "ffn_b": ln_ffn_b[:, None, :]}
    st = {"ckv": cache_mla_ckv[0], "kpe": cache_mla_kpe[0], "ret": state_ret[0],
          "gla_t": state_gla[0].swapaxes(-1, -2), "swa_k": cache_swa_k[0], "swa_v": cache_swa_v[0],
          "conv": [_conv_rows_to_chunks(state_ffn_conv[l]) for l in range(DEPTH)]}
    pos_p = jnp.arange(x_prompt.shape[1])
    pos_s = cache_mla_ckv.shape[2] + jnp.arange(x_sample.shape[1])
    yp = _run_trunk(x_prompt, pos_p, None, wts, ln, True)
    ys = _run_trunk(x_sample, pos_s, st, wts, ln, False)
    return (yp[0], ys[0]) + yp[1:] + ys[1:]
```
